```python
import jax, jax.numpy as jnp
from jax import lax
import numpy as np

D_MODEL = 1024
BATCH = 16
SEQ = 2048
DEPTH = 2

N_MIXERS = 2
N_A_LAYERS = (DEPTH + 1) // 2
N_B_LAYERS = DEPTH // 2
CONV_KERNEL = 31
POOL_WINDOWS = (2, 4, 8, 16)
N_POOL_GROUPS = len(POOL_WINDOWS)
POOL_GROUP_DIM = D_MODEL // N_POOL_GROUPS
D_FF = ((8 * D_MODEL // 3 + 127) // 128) * 128
FFN_CONV_KERNEL = 3
RMS_EPS = 1e-6
LN_EPS = 1e-5

kernel_name = "hybrid_conformerconv_msnpool_convffn"


def rmsnorm(x, g):
    xf = x.astype(jnp.float32)
    y = xf * lax.rsqrt(jnp.mean(xf * xf, axis=-1, keepdims=True) + RMS_EPS)
    return (y * g.astype(jnp.float32)).astype(x.dtype)


def layernorm(x, g, b):
    xf = x.astype(jnp.float32)
    mu = jnp.mean(xf, axis=-1, keepdims=True)
    var = jnp.mean(jnp.square(xf - mu), axis=-1, keepdims=True)
    y = (xf - mu) * lax.rsqrt(var + LN_EPS)
    return (y * g.astype(jnp.float32) + b.astype(jnp.float32)).astype(x.dtype)


def causal_dwconv(x, w, b):
    k, c = w.shape
    y = lax.conv_general_dilated(
        x, w[:, None, :].astype(x.dtype),
        window_strides=(1,), padding=((k - 1, 0),),
        dimension_numbers=("NWC", "WIO", "NWC"),
        feature_group_count=c)
    return y + b


def conformer_conv_module(h, w_pw1, b_pw1, w_dw, b_dw, ln_g, ln_b, w_pw2, b_pw2):
    d = h.shape[-1]
    a = jnp.einsum("bsd,de->bse", h, w_pw1) + b_pw1
    u = a[..., :d] * jax.nn.sigmoid(a[..., d:])
    u = causal_dwconv(u, w_dw, b_dw)
    u = jax.nn.silu(layernorm(u, ln_g, ln_b))
    return jnp.einsum("bsd,de->bse", u, w_pw2) + b_pw2


def multiscale_pool_mixer(h, w_grp, b_grp, scale):
    bsz, s, d = h.shape
    hf = h.astype(jnp.float32)
    cs = jnp.concatenate(
        [jnp.zeros((bsz, 1, d), jnp.float32), lax.cumsum(hf, axis=1)], axis=1)
    t = jnp.arange(s)
    outs = []
    for g, w in enumerate(POOL_WINDOWS):
        sl = slice(g * POOL_GROUP_DIM, (g + 1) * POOL_GROUP_DIM)
        c = cs[..., sl]
        lagged = jnp.pad(c[:, : s + 1 - w], ((0, 0), (w - 1, 0), (0, 0)))
        win_sum = c[:, 1:] - lagged
        cnt = jnp.minimum(t + 1, w).astype(jnp.float32)[None, :, None]
        outs.append(win_sum / cnt - hf[..., sl])
    pooled = jnp.stack(outs, axis=2).astype(h.dtype)
    mixed = jnp.einsum("bsgc,gce->bsge", pooled, w_grp).reshape(bsz, s, d)
    return scale * (mixed + b_grp)


def conv_ffn(h, w_up, w_dw, b_dw, w_down):
    up = jnp.einsum("bsd,df->bsf", h, w_up)
    act, gate = up[..., :D_FF], up[..., D_FF:]
    act = causal_dwconv(act, w_dw, b_dw)
    return jnp.einsum("bsf,fd->bsd", jax.nn.silu(act) * gate, w_down)


def setup_inputs(seed: int = 0) -> dict:
    key = jax.random.key(seed)
    ks = jax.random.split(key, 24)
    f32 = jnp.float32
    D, G, Cg = D_MODEL, N_POOL_GROUPS, POOL_GROUP_DIM
    nrm = lambda k, shape, sc: jax.random.normal(k, shape, f32) * sc
    gain = lambda k, shape: 1.0 + 0.05 * jax.random.normal(k, shape, f32)
    return {
        "x": jax.random.normal(ks[0], (BATCH, SEQ, D), f32),
        "norm_mix": gain(ks[1], (DEPTH, D)),
        "norm_ffn": gain(ks[2], (DEPTH, D)),
        "conv_w_pw1": nrm(ks[3], (N_A_LAYERS, D, 2 * D), D ** -0.5),
        "conv_b_pw1": nrm(ks[4], (N_A_LAYERS, 2 * D), 0.02),
        "conv_w_dw": nrm(ks[5], (N_A_LAYERS, CONV_KERNEL, D), CONV_KERNEL ** -0.5),
        "conv_b_dw": nrm(ks[6], (N_A_LAYERS, D), 0.02),
        "conv_ln_g": gain(ks[7], (N_A_LAYERS, D)),
        "conv_ln_b": nrm(ks[8], (N_A_LAYERS, D), 0.02),
        "conv_w_pw2": nrm(ks[9], (N_A_LAYERS, D, D), D ** -0.5),
        "conv_b_pw2": nrm(ks[10], (N_A_LAYERS, D), 0.02),
        "pool_w": nrm(ks[11], (N_B_LAYERS, G, Cg, Cg), Cg ** -0.5),
        "pool_b": nrm(ks[12], (N_B_LAYERS, D), 0.02),
        "pool_scale": gain(ks[13], (N_B_LAYERS, D)),
        "ffn_w_up": nrm(ks[14], (DEPTH, D, 2 * D_FF), D ** -0.5),
        "ffn_w_dw": nrm(ks[15], (DEPTH, FFN_CONV_KERNEL, D_FF), FFN_CONV_KERNEL ** -0.5),
        "ffn_b_dw": nrm(ks[16], (DEPTH, D_FF), 0.02),
        "ffn_w_down": nrm(ks[17], (DEPTH, D_FF, D), D_FF ** -0.5),
        "final_norm": gain(ks[18], (D,)),
    }


def reference(x, norm_mix, norm_ffn, conv_w_pw1, conv_b_pw1, conv_w_dw, conv_b_dw,
              conv_ln_g, conv_ln_b, conv_w_pw2, conv_b_pw2, pool_w, pool_b, pool_scale,
              ffn_w_up, ffn_w_dw, ffn_b_dw, ffn_w_down, final_norm):
    ia, ib = 0, 0
    for i in range(DEPTH):
        h = rmsnorm(x, norm_mix[i])
        if i % N_MIXERS == 0:
            x = x + conformer_conv_module(
                h, conv_w_pw1[ia], conv_b_pw1[ia], conv_w_dw[ia], conv_b_dw[ia],
                conv_ln_g[ia], conv_ln_b[ia], conv_w_pw2[ia], conv_b_pw2[ia])
            ia += 1
        else:
            x = x + multiscale_pool_mixer(h, pool_w[ib], pool_b[ib], pool_scale[ib])
            ib += 1
        h = rmsnorm(x, norm_ffn[i])
        x = x + conv_ffn(h, ffn_w_up[i], ffn_w_dw[i], ffn_b_dw[i], ffn_w_down[i])
    return rmsnorm(x, final_norm)
```

```python
import functools

import jax
import jax.numpy as jnp
from jax import lax
from jax.experimental import pallas as pl
from jax.experimental.pallas import tpu as pltpu

F32 = jnp.float32
BF16 = jnp.bfloat16

D_MODEL = 1024
D_FF = 2816
CONV_KERNEL = 31
FFN_CONV_KERNEL = 3
POOL_WINDOWS = (2, 4, 8, 16)
POOL_GROUP_DIM = D_MODEL // len(POOL_WINDOWS)
RMS_EPS = 1e-6
LN_EPS = 1e-5

SUBLANES = 8
LANES = 128
MXU_N = 256

SEQ_TILE = 512
FFN_CHUNK = MXU_N
N_FFN_CHUNKS = D_FF // FFN_CHUNK
N_LANE_BLOCKS = D_MODEL // LANES
CONV_HALO = 32
FFN_HALO = SUBLANES
POOL_HALO = 16
NORM_ROWS = 32
CONV_ROWS = 64
VMEM_LIMIT_BYTES = 56 * 1024 * 1024


def _row_block(i, rows):
    return pl.ds(pl.multiple_of(i * rows, rows), rows)


def _shifted_rows(start, rows):
    return pl.ds(start, rows, stride=1)


def _rmsnorm(x, g):
    ms = jnp.mean(x * x, axis=-1, keepdims=True)
    return (x * lax.rsqrt(ms + RMS_EPS)) * g


def _rmsnorm_to_bf16(load_rows, g_ref, dst_ref):
    def body(i, carry):
        r = _row_block(i, NORM_ROWS)
        dst_ref[r, :] = _rmsnorm(load_rows(r), g_ref[...]).astype(BF16)
        return carry
    lax.fori_loop(0, SEQ_TILE // NORM_ROWS, body, 0)


def _conv_ffn(h_ref, x1_ref, o_ref, w_up_a_ref, w_up_g_ref, w_dw_ref, b_dw_ref, w_down_ref,
              abuf, ahalo, gbuf, y_ref, finish):
    def chunk(c, carry):
        act = jnp.dot(h_ref[...], w_up_a_ref[c], preferred_element_type=F32)
        gbuf[...] = jnp.dot(h_ref[...], w_up_g_ref[c], preferred_element_type=F32)
        for lb in range(FFN_CHUNK // LANES):
            abuf[lb, 0:FFN_HALO, :] = ahalo[c, lb]
            abuf[lb, FFN_HALO:, :] = act[:, lb * LANES:(lb + 1) * LANES]
            ahalo[c, lb] = abuf[lb, SEQ_TILE:SEQ_TILE + FFN_HALO, :]
        w = w_dw_ref[c]
        b = b_dw_ref[c]

        def rows(i, carry2):
            base = pl.multiple_of(i * CONV_ROWS, CONV_ROWS)
            for lb in range(FFN_CHUNK // LANES):
                lanes = slice(lb * LANES, (lb + 1) * LANES)
                a = b[:, lanes]
                for k in range(FFN_CONV_KERNEL):
                    start = base + FFN_HALO - (FFN_CONV_KERNEL - 1) + k
                    a = a + w[k:k + 1, lanes] * abuf[lb, _shifted_rows(start, CONV_ROWS), :]
                y = jax.nn.silu(a) * gbuf[pl.ds(base, CONV_ROWS), lanes]
                y_ref[c, pl.ds(base, CONV_ROWS), lanes] = y.astype(BF16)
            return carry2
        lax.fori_loop(0, SEQ_TILE // CONV_ROWS, rows, 0)
        return carry
    lax.fori_loop(0, N_FFN_CHUNKS, chunk, 0)

    for j in range(D_MODEL // MXU_N):
        cols = slice(j * MXU_N, (j + 1) * MXU_N)
        acc = x1_ref[:, cols]
        for c in range(N_FFN_CHUNKS):
            acc = acc + jnp.dot(y_ref[c], w_down_ref[c, :, cols], preferred_element_type=F32)
        x1_ref[:, cols] = acc
    finish(x1_ref, o_ref)


def _store_tile(x2_ref, o_ref):
    o_ref[0] = x2_ref[...]


def _final_norm_store(g_ref, x2_ref, o_ref):
    def body(i, carry):
        r = _row_block(i, NORM_ROWS)
        o_ref[0, r, :] = _rmsnorm(x2_ref[r, :], g_ref[...])
        return carry
    lax.fori_loop(0, SEQ_TILE // NORM_ROWS, body, 0)


def _conformer_layer_kernel(
        x_ref, g_mix_ref, w_pw1_ref, b_pw1_ref, w_dw_ref, b_dw_ref, ln_g_ref, ln_b_ref,
        w_pw2_ref, b_pw2_ref,
        g_ffn_ref, w_up_a_ref, w_up_g_ref, w_dwf_ref, b_dwf_ref, w_down_ref,
        o_ref,
        h_ref, ubuf, cv_ref, x1_ref, abuf, ahalo, gbuf, y_ref):
    @pl.when(pl.program_id(1) == 0)
    def _():
        ubuf[:, 0:CONV_HALO, :] = jnp.zeros((N_LANE_BLOCKS, CONV_HALO, LANES), F32)
        ahalo[...] = jnp.zeros(ahalo.shape, F32)

    _rmsnorm_to_bf16(lambda r: x_ref[0, r, :], g_mix_ref, h_ref)
    for j in range(D_MODEL // MXU_N):
        cols = slice(j * MXU_N, (j + 1) * MXU_N)
        gcols = slice(D_MODEL + j * MXU_N, D_MODEL + (j + 1) * MXU_N)
        val = jnp.dot(h_ref[...], w_pw1_ref[:, cols], preferred_element_type=F32) + b_pw1_ref[:, cols]
        gate = jnp.dot(h_ref[...], w_pw1_ref[:, gcols], preferred_element_type=F32) + b_pw1_ref[:, gcols]
        u = val * jax.nn.sigmoid(gate)
        for lb in range(MXU_N // LANES):
            ubuf[j * (MXU_N // LANES) + lb, CONV_HALO:, :] = u[:, lb * LANES:(lb + 1) * LANES]

    for lb in range(N_LANE_BLOCKS):
        lanes = slice(lb * LANES, (lb + 1) * LANES)
        w = w_dw_ref[:, lanes]
        taps = [jnp.broadcast_to(w[k:k + 1, :], (SUBLANES, LANES)) for k in range(CONV_KERNEL)]
        bias = jnp.broadcast_to(b_dw_ref[:, lanes], (SUBLANES, LANES))

        def rows(i, carry, lb=lb, lanes=lanes, taps=taps, bias=bias):
            base = pl.multiple_of(i * CONV_ROWS, CONV_ROWS)
            for s in range(CONV_ROWS // SUBLANES):
                acc = bias
                for k in range(CONV_KERNEL):
                    start = base + s * SUBLANES + CONV_HALO - (CONV_KERNEL - 1) + k
                    acc = acc + taps[k] * ubuf[lb, _shifted_rows(start, SUBLANES), :]
                cv_ref[pl.ds(pl.multiple_of(base + s * SUBLANES, SUBLANES), SUBLANES), lanes] = acc
            return carry
        lax.fori_loop(0, SEQ_TILE // CONV_ROWS, rows, 0)
    ubuf[:, 0:CONV_HALO, :] = ubuf[:, SEQ_TILE:SEQ_TILE + CONV_HALO, :]

    def ln_rows(i, carry):
        r = _row_block(i, NORM_ROWS)
        v = cv_ref[r, :]
        mu = jnp.mean(v, axis=-1, keepdims=True)
        var = jnp.mean(jnp.square(v - mu), axis=-1, keepdims=True)
        y = (v - mu) * lax.rsqrt(var + LN_EPS)
        h_ref[r, :] = jax.nn.silu(y * ln_g_ref[...] + ln_b_ref[...]).astype(BF16)
        return carry
    lax.fori_loop(0, SEQ_TILE // NORM_ROWS, ln_rows, 0)

    for j in range(D_MODEL // MXU_N):
        cols = slice(j * MXU_N, (j + 1) * MXU_N)
        x1_ref[:, cols] = (x_ref[0, :, cols] + b_pw2_ref[:, cols]
                           + jnp.dot(h_ref[...], w_pw2_ref[:, cols], preferred_element_type=F32))

    _rmsnorm_to_bf16(lambda r: x1_ref[r, :], g_ffn_ref, h_ref)
    _conv_ffn(h_ref, x1_ref, o_ref, w_up_a_ref, w_up_g_ref, w_dwf_ref, b_dwf_ref, w_down_ref,
              abuf, ahalo, gbuf, y_ref, _store_tile)


def _pool_layer_kernel(
        x_ref, g_mix_ref, w_pool_ref, b_pool_ref, scale_ref,
        g_ffn_ref, w_up_a_ref, w_up_g_ref, w_dwf_ref, b_dwf_ref, w_down_ref, g_final_ref,
        o_ref,
        h_ref, hbuf, x1_ref, abuf, ahalo, gbuf, y_ref):
    seq_tile = pl.program_id(1)

    @pl.when(seq_tile == 0)
    def _():
        hbuf[:, 0:POOL_HALO, :] = jnp.zeros((N_LANE_BLOCKS, POOL_HALO, LANES), F32)
        ahalo[...] = jnp.zeros(ahalo.shape, F32)

    def norm_rows(i, carry):
        r = _row_block(i, NORM_ROWS)
        h = _rmsnorm(x_ref[0, r, :], g_mix_ref[...])
        for lb in range(N_LANE_BLOCKS):
            hbuf[lb, pl.ds(pl.multiple_of(i * NORM_ROWS + POOL_HALO, SUBLANES), NORM_ROWS), :] = (
                h[:, lb * LANES:(lb + 1) * LANES])
        return carry
    lax.fori_loop(0, SEQ_TILE // NORM_ROWS, norm_rows, 0)

    lanes_per_group = POOL_GROUP_DIM // LANES

    def pool_rows(i, carry):
        base = pl.multiple_of(i * NORM_ROWS, NORM_ROWS)
        t = seq_tile * SEQ_TILE + base + lax.broadcasted_iota(jnp.int32, (NORM_ROWS, LANES), 0)
        for lb in range(N_LANE_BLOCKS):
            win = POOL_WINDOWS[lb // lanes_per_group]
            cur = hbuf[lb, pl.ds(pl.multiple_of(base + POOL_HALO, SUBLANES), NORM_ROWS), :]
            tot = cur
            for back in range(1, win):
                tot = tot + hbuf[lb, _shifted_rows(base + POOL_HALO - back, NORM_ROWS), :]
            cnt = jnp.minimum(t + 1, win).astype(F32)
            h_ref[pl.ds(base, NORM_ROWS), lb * LANES:(lb + 1) * LANES] = (tot / cnt - cur).astype(BF16)
        return carry
    lax.fori_loop(0, SEQ_TILE // NORM_ROWS, pool_rows, 0)
    hbuf[:, 0:POOL_HALO, :] = hbuf[:, SEQ_TILE:SEQ_TILE + POOL_HALO, :]

    for g in range(len(POOL_WINDOWS)):
        cols = slice(g * POOL_GROUP_DIM, (g + 1) * POOL_GROUP_DIM)
        mixed = jnp.dot(h_ref[:, cols], w_pool_ref[g], preferred_element_type=F32)
        x1_ref[:, cols] = x_ref[0, :, cols] + scale_ref[:, cols] * (mixed + b_pool_ref[:, cols])

    _rmsnorm_to_bf16(lambda r: x1_ref[r, :], g_ffn_ref, h_ref)
    _conv_ffn(h_ref, x1_ref, o_ref, w_up_a_ref, w_up_g_ref, w_dwf_ref, b_dwf_ref, w_down_ref,
              abuf, ahalo, gbuf, y_ref, functools.partial(_final_norm_store, g_final_ref))


def _resident(shape):
    zeros = (0,) * len(shape)
    return pl.BlockSpec(shape, lambda b, s: zeros, pipeline_mode=pl.Buffered(1))


def _ffn_scratch():
    return [
        pltpu.VMEM((FFN_CHUNK // LANES, FFN_HALO + SEQ_TILE, LANES), F32),
        pltpu.VMEM((N_FFN_CHUNKS, FFN_CHUNK // LANES, FFN_HALO, LANES), F32),
        pltpu.VMEM((SEQ_TILE, FFN_CHUNK), F32),
        pltpu.VMEM((N_FFN_CHUNKS, SEQ_TILE, FFN_CHUNK), BF16),
    ]


def _ffn_operands(w_up, w_dw, b_dw, w_down):
    def chunked_cols(w):
        return w.reshape(D_MODEL, N_FFN_CHUNKS, FFN_CHUNK).transpose(1, 0, 2).astype(BF16)
    w_up_a = chunked_cols(w_up[:, :D_FF])
    w_up_g = chunked_cols(w_up[:, D_FF:])
    w_dwc = w_dw.reshape(FFN_CONV_KERNEL, N_FFN_CHUNKS, FFN_CHUNK).transpose(1, 0, 2)
    b_dwc = b_dw.reshape(N_FFN_CHUNKS, 1, FFN_CHUNK)
    w_downc = w_down.reshape(N_FFN_CHUNKS, FFN_CHUNK, D_MODEL).astype(BF16)
    return w_up_a, w_up_g, w_dwc, b_dwc, w_downc


def _layer_call(kernel_fn, x, operands, scratch_shapes, name):
    batch, seq, _ = x.shape
    tile = pl.BlockSpec((1, SEQ_TILE, D_MODEL), lambda b, s: (b, s, 0))
    return pl.pallas_call(
        kernel_fn,
        grid=(batch, seq // SEQ_TILE),
        in_specs=[tile] + [_resident(op.shape) for op in operands],
        out_specs=tile,
        out_shape=jax.ShapeDtypeStruct(x.shape, x.dtype),
        scratch_shapes=scratch_shapes,
        compiler_params=pltpu.CompilerParams(
            dimension_semantics=("arbitrary", "arbitrary"),
            vmem_limit_bytes=VMEM_LIMIT_BYTES),
        name=name,
    )(x, *operands)


def kernel(x, norm_mix, norm_ffn, conv_w_pw1, conv_b_pw1, conv_w_dw, conv_b_dw, conv_ln_g, conv_ln_b,
           conv_w_pw2, conv_b_pw2, pool_w, pool_b, pool_scale, ffn_w_up, ffn_w_dw, ffn_b_dw, ffn_w_down,
           final_norm):
    assert x.shape[1] % SEQ_TILE == 0 and x.shape[2] == D_MODEL
    row = lambda v: v.reshape(1, -1)

    conformer_operands = (
        row(norm_mix[0]), conv_w_pw1[0].astype(BF16), row(conv_b_pw1[0]), conv_w_dw[0], row(conv_b_dw[0]),
        row(conv_ln_g[0]), row(conv_ln_b[0]), conv_w_pw2[0].astype(BF16), row(conv_b_pw2[0]),
        row(norm_ffn[0]), *_ffn_operands(ffn_w_up[0], ffn_w_dw[0], ffn_b_dw[0], ffn_w_down[0]))
    x = _layer_call(
        _conformer_layer_kernel, x, conformer_operands,
        [pltpu.VMEM((SEQ_TILE, D_MODEL), BF16),
         pltpu.VMEM((N_LANE_BLOCKS, CONV_HALO + SEQ_TILE, LANES), F32),
         pltpu.VMEM((SEQ_TILE, D_MODEL), F32),
         pltpu.VMEM((SEQ_TILE, D_MODEL), F32),
         ] + _ffn_scratch(),
        "conformer_layer")

    pool_operands = (
        row(norm_mix[1]), pool_w[0].astype(BF16), row(pool_b[0]), row(pool_scale[0]),
        row(norm_ffn[1]), *_ffn_operands(ffn_w_up[1], ffn_w_dw[1], ffn_b_dw[1], ffn_w_down[1]),
        row(final_norm))
    return _layer_call(
        _pool_layer_kernel, x, pool_operands,
        [pltpu.VMEM((SEQ_TILE, D_MODEL), BF16),
         pltpu.VMEM((N_LANE_BLOCKS, POOL_HALO + SEQ_TILE, LANES), F32),
         pltpu.VMEM((SEQ_TILE, D_MODEL), F32),
         ] + _ffn_scratch(),
        "pool_layer")
```

```python
import functools

import jax
import jax.numpy as jnp
from jax import lax
from jax.experimental import pallas as pl
from jax.experimental.pallas import tpu as pltpu

F32 = jnp.float32
BF16 = jnp.bfloat16
U32 = jnp.uint32

D_MODEL = 1024
D_FF = 2816
CONV_KERNEL = 31
FFN_CONV_KERNEL = 3
POOL_WINDOWS = (2, 4, 8, 16)
POOL_GROUP_DIM = D_MODEL // len(POOL_WINDOWS)
RMS_EPS = 1e-6
LN_EPS = 1e-5

SUBLANES = 8
LANES = 128
PACKED_ROWS = 2 * SUBLANES
MXU_N = 256

SEQ_TILE = 512
FFN_CHUNK = MXU_N
N_FFN_CHUNKS = D_FF // FFN_CHUNK
FFN_LANE_BLOCKS = FFN_CHUNK // LANES
N_LANE_BLOCKS = D_MODEL // LANES
CONV_HALO = 32
FFN_HALO = SUBLANES
POOL_HALO = 16
CONV_ROWS = 64
N_ACT_BUFS = 3
VMEM_LIMIT_BYTES = 58 * 1024 * 1024


def _rows(start, n):
    return pl.ds(start, n, stride=1)


def _bcast_row(k):
    return pl.ds(k, SUBLANES, stride=0)


def _lanes(lb):
    return slice(lb * LANES, (lb + 1) * LANES)


def _pack_bf16(x):
    return pltpu.bitcast(x.astype(BF16), U32)


def _packed_as_bf16(ref_or_val):
    return pltpu.bitcast(ref_or_val, BF16)


def _rmsnorm(x, g):
    ms = jnp.mean(x * x, axis=-1, keepdims=True)
    return (x * lax.rsqrt(ms + RMS_EPS)) * g


def _rmsnorm_packed(load_rows, g_ref, dst_ref):
    for r in range(SEQ_TILE // PACKED_ROWS):
        x = load_rows(slice(r * PACKED_ROWS, (r + 1) * PACKED_ROWS))
        dst_ref[r * SUBLANES:(r + 1) * SUBLANES, :] = _pack_bf16(_rmsnorm(x, g_ref[...]))


def _conv_ffn(first_tile, h2p_ref, x1_ref, w_up_a_ref, w_up_g_ref, w_dw_ref, w_down_ref,
              act_bufs, gate_bufs, ahalo, y_ref, store_cols):
    for c in range(N_FFN_CHUNKS):
        abuf = act_bufs[c % N_ACT_BUFS]
        gbuf = gate_bufs[c % N_ACT_BUFS]
        act = jnp.dot(_packed_as_bf16(h2p_ref[...]), w_up_a_ref[c], preferred_element_type=F32)
        gbuf[...] = jnp.dot(_packed_as_bf16(h2p_ref[...]), w_up_g_ref[c], preferred_element_type=F32)
        for lb in range(FFN_LANE_BLOCKS):
            abuf[lb, 0:FFN_HALO, :] = jnp.where(first_tile, 0.0, ahalo[c, lb])
            abuf[lb, FFN_HALO:, :] = act[:, _lanes(lb)]
            ahalo[c, lb] = abuf[lb, SEQ_TILE:SEQ_TILE + FFN_HALO, :]
            taps = [w_dw_ref[c, lb, _bcast_row(k), :] for k in range(FFN_CONV_KERNEL)]
            bias = w_dw_ref[c, lb, _bcast_row(FFN_CONV_KERNEL), :]
            for r in range(SEQ_TILE // PACKED_ROWS):
                halves = []
                for half in range(2):
                    r0 = r * PACKED_ROWS + half * SUBLANES
                    a = bias
                    for k in range(FFN_CONV_KERNEL):
                        start = r0 + FFN_HALO - (FFN_CONV_KERNEL - 1) + k
                        a = a + taps[k] * abuf[lb, _rows(start, SUBLANES), :]
                    halves.append(jax.nn.silu(a) * gbuf[r0:r0 + SUBLANES, _lanes(lb)])
                y_ref[c, r * SUBLANES:(r + 1) * SUBLANES, _lanes(lb)] = _pack_bf16(
                    jnp.concatenate(halves, axis=0))

    for j in range(D_MODEL // MXU_N):
        cols = slice(j * MXU_N, (j + 1) * MXU_N)
        acc = x1_ref[:, cols]
        for c in range(N_FFN_CHUNKS):
            acc = acc + jnp.dot(_packed_as_bf16(y_ref[c]), w_down_ref[c, :, cols],
                                preferred_element_type=F32)
        store_cols(cols, acc)


def _zero(ref):
    ref[...] = jnp.zeros(ref.shape, ref.dtype)


def _conformer_layer_kernel(
        tiles_per_seq,
        x_ref, g_mix_ref, w_pw1_ref, b_pw1_ref, w_dw_ref, ln_g_ref, ln_b_ref, w_pw2_ref, b_pw2_ref,
        g_ffn_ref, w_up_a_ref, w_up_g_ref, w_dwf_ref, w_down_ref,
        o_ref,
        hp_ref, ubuf, cv_ref, x1_ref, h2p_ref, ahalo, y_ref, *act_gate_bufs):
    act_bufs, gate_bufs = act_gate_bufs[:N_ACT_BUFS], act_gate_bufs[N_ACT_BUFS:]
    g = pl.program_id(0)
    first_front = g % tiles_per_seq == 0
    first_back = (g + tiles_per_seq - 1) % tiles_per_seq == 0

    @pl.when(g == 0)
    def _():
        _zero(ubuf)
        _zero(x1_ref)
        _zero(h2p_ref)
        _zero(ahalo)

    def store_out(cols, val):
        o_ref[0, :, cols] = val
    _conv_ffn(first_back, h2p_ref, x1_ref, w_up_a_ref, w_up_g_ref, w_dwf_ref, w_down_ref,
              act_bufs, gate_bufs, ahalo, y_ref, store_out)

    for lb in range(N_LANE_BLOCKS):
        ubuf[lb, 0:CONV_HALO, :] = jnp.where(
            first_front, 0.0, ubuf[lb, SEQ_TILE:SEQ_TILE + CONV_HALO, :])

    _rmsnorm_packed(lambda r: x_ref[0, r, :], g_mix_ref, hp_ref)
    for j in range(D_MODEL // MXU_N):
        cols = slice(j * MXU_N, (j + 1) * MXU_N)
        gcols = slice(D_MODEL + j * MXU_N, D_MODEL + (j + 1) * MXU_N)
        val = jnp.dot(_packed_as_bf16(hp_ref[...]), w_pw1_ref[:, cols],
                      preferred_element_type=F32) + b_pw1_ref[:, cols]
        gate = jnp.dot(_packed_as_bf16(hp_ref[...]), w_pw1_ref[:, gcols],
                       preferred_element_type=F32) + b_pw1_ref[:, gcols]
        u = val * jax.nn.sigmoid(gate)
        for lb in range(MXU_N // LANES):
            ubuf[j * (MXU_N // LANES) + lb, CONV_HALO:, :] = u[:, _lanes(lb)]

    for lb in range(N_LANE_BLOCKS):
        for rb in range(SEQ_TILE // CONV_ROWS):
            base = rb * CONV_ROWS
            n_sub = CONV_ROWS // SUBLANES
            accs = [w_dw_ref[lb, _bcast_row(CONV_KERNEL), :]] * n_sub
            for k in range(CONV_KERNEL):
                tap = w_dw_ref[lb, _bcast_row(k), :]
                for s in range(n_sub):
                    start = base + s * SUBLANES + CONV_HALO - (CONV_KERNEL - 1) + k
                    accs[s] = accs[s] + tap * ubuf[lb, _rows(start, SUBLANES), :]
            for s in range(n_sub):
                r0 = base + s * SUBLANES
                cv_ref[r0:r0 + SUBLANES, _lanes(lb)] = accs[s]

    for r in range(SEQ_TILE // PACKED_ROWS):
        v = cv_ref[r * PACKED_ROWS:(r + 1) * PACKED_ROWS, :]
        mu = jnp.mean(v, axis=-1, keepdims=True)
        var = jnp.mean(jnp.square(v - mu), axis=-1, keepdims=True)
        y = (v - mu) * lax.rsqrt(var + LN_EPS)
        hp_ref[r * SUBLANES:(r + 1) * SUBLANES, :] = _pack_bf16(
            jax.nn.silu(y * ln_g_ref[...] + ln_b_ref[...]))

    for j in range(D_MODEL // MXU_N):
        cols = slice(j * MXU_N, (j + 1) * MXU_N)
        x1_ref[:, cols] = (x_ref[0, :, cols] + b_pw2_ref[:, cols]
                           + jnp.dot(_packed_as_bf16(hp_ref[...]), w_pw2_ref[:, cols],
                                     preferred_element_type=F32))
    _rmsnorm_packed(lambda r: x1_ref[r, :], g_ffn_ref, h2p_ref)


def _pool_layer_kernel(
        tiles_per_seq,
        x_ref, g_mix_ref, w_pool_ref, b_pool_ref, scale_ref,
        g_ffn_ref, w_up_a_ref, w_up_g_ref, w_dwf_ref, w_down_ref, g_final_ref,
        o_ref,
        pp_ref, hbuf, x2_ref, x1_ref, h2p_ref, ahalo, y_ref, *act_gate_bufs):
    act_bufs, gate_bufs = act_gate_bufs[:N_ACT_BUFS], act_gate_bufs[N_ACT_BUFS:]
    g = pl.program_id(0)
    seq_tile = g % tiles_per_seq
    first_front = seq_tile == 0
    first_back = (g + tiles_per_seq - 1) % tiles_per_seq == 0

    @pl.when(g == 0)
    def _():
        _zero(hbuf)
        _zero(x1_ref)
        _zero(h2p_ref)
        _zero(ahalo)

    def store_x2(cols, val):
        x2_ref[:, cols] = val
    _conv_ffn(first_back, h2p_ref, x1_ref, w_up_a_ref, w_up_g_ref, w_dwf_ref, w_down_ref,
              act_bufs, gate_bufs, ahalo, y_ref, store_x2)
    for r in range(SEQ_TILE // PACKED_ROWS):
        rows = slice(r * PACKED_ROWS, (r + 1) * PACKED_ROWS)
        o_ref[0, rows, :] = _rmsnorm(x2_ref[rows, :], g_final_ref[...])

    for lb in range(N_LANE_BLOCKS):
        hbuf[lb, 0:POOL_HALO, :] = jnp.where(
            first_front, 0.0, hbuf[lb, SEQ_TILE:SEQ_TILE + POOL_HALO, :])
    for r in range(SEQ_TILE // PACKED_ROWS):
        rows = slice(r * PACKED_ROWS, (r + 1) * PACKED_ROWS)
        h = _rmsnorm(x_ref[0, rows, :], g_mix_ref[...])
        for lb in range(N_LANE_BLOCKS):
            hbuf[lb, POOL_HALO + r * PACKED_ROWS:POOL_HALO + (r + 1) * PACKED_ROWS, :] = h[:, _lanes(lb)]

    lanes_per_group = POOL_GROUP_DIM // LANES
    row_in_tile = lax.broadcasted_iota(jnp.int32, (PACKED_ROWS, LANES), 0)
    for r in range(SEQ_TILE // PACKED_ROWS):
        t = seq_tile * SEQ_TILE + r * PACKED_ROWS + row_in_tile
        for lb in range(N_LANE_BLOCKS):
            win = POOL_WINDOWS[lb // lanes_per_group]
            base = POOL_HALO + r * PACKED_ROWS
            cur = hbuf[lb, base:base + PACKED_ROWS, :]
            tot = cur
            for back in range(1, win):
                tot = tot + hbuf[lb, _rows(base - back, PACKED_ROWS), :]
            cnt = jnp.minimum(t + 1, win).astype(F32)
            pp_ref[r * SUBLANES:(r + 1) * SUBLANES, _lanes(lb)] = _pack_bf16(tot / cnt - cur)

    for grp in range(len(POOL_WINDOWS)):
        cols = slice(grp * POOL_GROUP_DIM, (grp + 1) * POOL_GROUP_DIM)
        mixed = jnp.dot(_packed_as_bf16(pp_ref[:, cols]), w_pool_ref[grp], preferred_element_type=F32)
        x1_ref[:, cols] = x_ref[0, :, cols] + scale_ref[:, cols] * (mixed + b_pool_ref[:, cols])
    _rmsnorm_packed(lambda r: x1_ref[r, :], g_ffn_ref, h2p_ref)


def _resident(shape):
    zeros = (0,) * len(shape)
    return pl.BlockSpec(shape, lambda g: zeros, pipeline_mode=pl.Buffered(1))


def _ffn_scratch():
    return [
        pltpu.VMEM((SEQ_TILE, D_MODEL), F32),
        pltpu.VMEM((SEQ_TILE // 2, D_MODEL), U32),
        pltpu.VMEM((N_FFN_CHUNKS, FFN_LANE_BLOCKS, FFN_HALO, LANES), F32),
        pltpu.VMEM((N_FFN_CHUNKS, SEQ_TILE // 2, FFN_CHUNK), U32),
    ] + [pltpu.VMEM((FFN_LANE_BLOCKS, FFN_HALO + SEQ_TILE, LANES), F32)] * N_ACT_BUFS \
      + [pltpu.VMEM((SEQ_TILE, FFN_CHUNK), F32)] * N_ACT_BUFS


def _lane_blocked_rows(rows):
    n, c = rows.shape
    pad = -n % SUBLANES
    rows = jnp.pad(rows, ((0, pad), (0, 0)))
    return rows.reshape(n + pad, c // LANES, LANES).transpose(1, 0, 2)


def _ffn_operands(w_up, w_dw, b_dw, w_down):
    def chunked_cols(w):
        return w.reshape(D_MODEL, N_FFN_CHUNKS, FFN_CHUNK).transpose(1, 0, 2).astype(BF16)
    w_up_a = chunked_cols(w_up[:, :D_FF])
    w_up_g = chunked_cols(w_up[:, D_FF:])
    taps = _lane_blocked_rows(jnp.concatenate([w_dw, b_dw[None, :]], axis=0))
    taps = taps.reshape(N_FFN_CHUNKS, FFN_LANE_BLOCKS, SUBLANES, LANES)
    w_downc = w_down.reshape(N_FFN_CHUNKS, FFN_CHUNK, D_MODEL).astype(BF16)
    return w_up_a, w_up_g, taps, w_downc


def _layer_call(kernel_fn, x, operands, scratch_shapes, name):
    batch, seq, _ = x.shape
    tiles_per_seq = seq // SEQ_TILE
    n_tiles = batch * tiles_per_seq
    x_tiles = x.reshape(n_tiles, SEQ_TILE, D_MODEL)
    out = pl.pallas_call(
        functools.partial(kernel_fn, tiles_per_seq),
        grid=(n_tiles + 1,),
        in_specs=[pl.BlockSpec((1, SEQ_TILE, D_MODEL), lambda g: (jnp.minimum(g, n_tiles - 1), 0, 0))]
        + [_resident(op.shape) for op in operands],
        out_specs=pl.BlockSpec((1, SEQ_TILE, D_MODEL), lambda g: (jnp.maximum(g - 1, 0), 0, 0)),
        out_shape=jax.ShapeDtypeStruct(x_tiles.shape, x.dtype),
        scratch_shapes=scratch_shapes,
        compiler_params=pltpu.CompilerParams(
            dimension_semantics=("arbitrary",),
            vmem_limit_bytes=VMEM_LIMIT_BYTES),
        name=name,
    )(x_tiles, *operands)
    return out.reshape(x.shape)


def kernel(x, norm_mix, norm_ffn, conv_w_pw1, conv_b_pw1, conv_w_dw, conv_b_dw, conv_ln_g, conv_ln_b,
           conv_w_pw2, conv_b_pw2, pool_w, pool_b, pool_scale, ffn_w_up, ffn_w_dw, ffn_b_dw, ffn_w_down,
           final_norm):
    assert x.shape[1] % SEQ_TILE == 0 and x.shape[2] == D_MODEL
    row = lambda v: v.reshape(1, -1)

    conv_taps = _lane_blocked_rows(jnp.concatenate([conv_w_dw[0], conv_b_dw[0][None, :]], axis=0))
    conformer_operands = (
        row(norm_mix[0]), conv_w_pw1[0].astype(BF16), row(conv_b_pw1[0]), conv_taps,
        row(conv_ln_g[0]), row(conv_ln_b[0]), conv_w_pw2[0].astype(BF16), row(conv_b_pw2[0]),
        row(norm_ffn[0]), *_ffn_operands(ffn_w_up[0], ffn_w_dw[0], ffn_b_dw[0], ffn_w_down[0]))
    x = _layer_call(
        _conformer_layer_kernel, x, conformer_operands,
        [pltpu.VMEM((SEQ_TILE // 2, D_MODEL), U32),
         pltpu.VMEM((N_LANE_BLOCKS, CONV_HALO + SEQ_TILE, LANES), F32),
         pltpu.VMEM((SEQ_TILE, D_MODEL), F32),
         ] + _ffn_scratch(),
        "conformer_layer")

    pool_operands = (
        row(norm_mix[1]), pool_w[0].astype(BF16), row(pool_b[0]), row(pool_scale[0]),
        row(norm_ffn[1]), *_ffn_operands(ffn_w_up[1], ffn_w_dw[1], ffn_b_dw[1], ffn_w_down[1]),
        row(final_norm))
    return _layer_call(
        _pool_layer_kernel, x, pool_operands,
        [pltpu.VMEM((SEQ_TILE // 2, D_MODEL), U32),
         pltpu.VMEM((N_LANE_BLOCKS, POOL_HALO + SEQ_TILE, LANES), F32),
         pltpu.VMEM((SEQ_TILE, D_MODEL), F32),
         ] + _ffn_scratch(),
        "pool_layer")
```

```python
import functools

import jax
import jax.numpy as jnp
from jax import lax
from jax.experimental import pallas as pl
from jax.experimental.pallas import tpu as pltpu

F32 = jnp.float32
BF16 = jnp.bfloat16
U32 = jnp.uint32

D_MODEL = 1024
D_FF = 2816
CONV_KERNEL = 31
FFN_CONV_KERNEL = 3
POOL_WINDOWS = (2, 4, 8, 16)
POOL_GROUP_DIM = D_MODEL // len(POOL_WINDOWS)
RMS_EPS = 1e-6
LN_EPS = 1e-5

SUBLANES = 8
LANES = 128
PACKED_ROWS = 2 * SUBLANES
MXU_N = 256

SEQ_TILE = 512
FFN_CHUNK = MXU_N
N_FFN_CHUNKS = D_FF // FFN_CHUNK
FFN_LANE_BLOCKS = FFN_CHUNK // LANES
N_LANE_BLOCKS = D_MODEL // LANES
CONV_HALO = 32
FFN_HALO = SUBLANES
POOL_HALO = 16
CONV_ROWS = 32
CONV_CHAINS = 2
N_ACT_BUFS = 3
VMEM_LIMIT_BYTES = 58 * 1024 * 1024


def _rows(start, n):
    return pl.ds(start, n, stride=1)


def _bcast_row(k):
    return pl.ds(k, SUBLANES, stride=0)


def _lanes(lb):
    return slice(lb * LANES, (lb + 1) * LANES)


def _pack_bf16(x):
    return pltpu.bitcast(x.astype(BF16), U32)


def _packed_as_bf16(ref_or_val):
    return pltpu.bitcast(ref_or_val, BF16)


def _rmsnorm(x, g):
    ms = jnp.mean(x * x, axis=-1, keepdims=True)
    return (x * lax.rsqrt(ms + RMS_EPS)) * g


def _rmsnorm_packed(load_rows, g_ref, dst_ref):
    for r in range(SEQ_TILE // PACKED_ROWS):
        x = load_rows(slice(r * PACKED_ROWS, (r + 1) * PACKED_ROWS))
        dst_ref[r * SUBLANES:(r + 1) * SUBLANES, :] = _pack_bf16(_rmsnorm(x, g_ref[...]))


def _conv_ffn(first_tile, h2p_ref, x1_ref, w_up_ref, w_dw_ref, w_down_ref,
              act_bufs, gate_bufs, ahalo, y_ref, store_cols):
    for c in range(N_FFN_CHUNKS):
        abuf = act_bufs[c % N_ACT_BUFS]
        gbuf = gate_bufs[c % N_ACT_BUFS]
        chunk = slice(c * FFN_CHUNK, (c + 1) * FFN_CHUNK)
        gate_chunk = slice(D_FF + c * FFN_CHUNK, D_FF + (c + 1) * FFN_CHUNK)
        act = jnp.dot(_packed_as_bf16(h2p_ref[...]), w_up_ref[:, chunk], preferred_element_type=F32)
        gbuf[...] = jnp.dot(_packed_as_bf16(h2p_ref[...]), w_up_ref[:, gate_chunk],
                            preferred_element_type=F32)
        for lb in range(FFN_LANE_BLOCKS):
            abuf[lb, 0:FFN_HALO, :] = jnp.where(first_tile, 0.0, ahalo[c, lb])
            abuf[lb, FFN_HALO:, :] = act[:, _lanes(lb)]
            ahalo[c, lb] = abuf[lb, SEQ_TILE:SEQ_TILE + FFN_HALO, :]
            taps = [w_dw_ref[c, lb, _bcast_row(k), :] for k in range(FFN_CONV_KERNEL)]
            bias = w_dw_ref[c, lb, _bcast_row(FFN_CONV_KERNEL), :]
            for r in range(SEQ_TILE // PACKED_ROWS):
                halves = []
                for half in range(2):
                    r0 = r * PACKED_ROWS + half * SUBLANES
                    a = bias
                    for k in range(FFN_CONV_KERNEL):
                        start = r0 + FFN_HALO - (FFN_CONV_KERNEL - 1) + k
                        a = a + taps[k] * abuf[lb, _rows(start, SUBLANES), :]
                    halves.append(jax.nn.silu(a) * gbuf[r0:r0 + SUBLANES, _lanes(lb)])
                y_ref[c, r * SUBLANES:(r + 1) * SUBLANES, _lanes(lb)] = _pack_bf16(
                    jnp.concatenate(halves, axis=0))

    for j in range(D_MODEL // MXU_N):
        cols = slice(j * MXU_N, (j + 1) * MXU_N)
        acc = x1_ref[:, cols]
        for c in range(N_FFN_CHUNKS):
            acc = acc + jnp.dot(_packed_as_bf16(y_ref[c]),
                                w_down_ref[c * FFN_CHUNK:(c + 1) * FFN_CHUNK, cols],
                                preferred_element_type=F32)
        store_cols(cols, acc)


def _after(x, token, zero_bits):
    dep = pltpu.bitcast(pltpu.roll(token, 1, axis=1), U32) & zero_bits
    return pltpu.bitcast(pltpu.bitcast(x, U32) | dep, F32)


def _zero(ref):
    ref[...] = jnp.zeros(ref.shape, ref.dtype)


def _conformer_layer_kernel(
        tiles_per_seq,
        xf_ref, xm_ref, zero_ref, g_mix_ref, w_pw1_ref, b_pw1_ref, w_dw_ref, ln_g_ref, ln_b_ref,
        w_pw2_ref, b_pw2_ref,
        g_ffn_ref, w_up_ref, w_dwf_ref, w_down_ref,
        o_ref,
        hp1_ref, hpv_ref, ubuf, cv_ref, x1_ref, h2p_ref, ahalo, y_ref, *act_gate_bufs):
    act_bufs, gate_bufs = act_gate_bufs[:N_ACT_BUFS], act_gate_bufs[N_ACT_BUFS:]
    g = pl.program_id(0)
    first_front = g % tiles_per_seq == 0
    first_back = (g + 2 * tiles_per_seq - 2) % tiles_per_seq == 0
    cur = g % 2
    prev = 1 - cur

    @pl.when(g == 0)
    def _():
        _zero(ubuf)
        _zero(x1_ref)
        _zero(h2p_ref)
        _zero(ahalo)

    def store_out(cols, val):
        o_ref[0, :, cols] = val
    _conv_ffn(first_back, h2p_ref, x1_ref, w_up_ref, w_dwf_ref, w_down_ref,
              act_bufs, gate_bufs, ahalo, y_ref, store_out)

    tokens = []
    for rb in range(SEQ_TILE // CONV_ROWS):
        for lb in range(N_LANE_BLOCKS):
            base = rb * CONV_ROWS
            n_sub = CONV_ROWS // SUBLANES
            bias = w_dw_ref[lb, _bcast_row(CONV_KERNEL), :]
            if len(tokens) >= CONV_CHAINS:
                bias = _after(bias, tokens[-CONV_CHAINS], zero_ref[...])
            accs = [bias] * n_sub
            for k in range(CONV_KERNEL):
                tap = w_dw_ref[lb, _bcast_row(k), :]
                for s in range(n_sub):
                    start = base + s * SUBLANES + CONV_HALO - (CONV_KERNEL - 1) + k
                    accs[s] = accs[s] + tap * ubuf[prev, lb, _rows(start, SUBLANES), :]
            token = accs[0]
            for s in range(n_sub):
                r0 = base + s * SUBLANES
                cv_ref[r0:r0 + SUBLANES, _lanes(lb)] = accs[s]
                if s:
                    token = jnp.maximum(token, accs[s])
            tokens.append(token)

    for r in range(SEQ_TILE // PACKED_ROWS):
        v = cv_ref[r * PACKED_ROWS:(r + 1) * PACKED_ROWS, :]
        mu = jnp.mean(v, axis=-1, keepdims=True)
        var = jnp.mean(jnp.square(v - mu), axis=-1, keepdims=True)
        y = (v - mu) * lax.rsqrt(var + LN_EPS)
        hpv_ref[r * SUBLANES:(r + 1) * SUBLANES, :] = _pack_bf16(
            jax.nn.silu(y * ln_g_ref[...] + ln_b_ref[...]))

    for j in range(D_MODEL // MXU_N):
        cols = slice(j * MXU_N, (j + 1) * MXU_N)
        x1_ref[:, cols] = (xm_ref[0, :, cols] + b_pw2_ref[:, cols]
                           + jnp.dot(_packed_as_bf16(hpv_ref[...]), w_pw2_ref[:, cols],
                                     preferred_element_type=F32))
    _rmsnorm_packed(lambda r: x1_ref[r, :], g_ffn_ref, h2p_ref)

    for lb in range(N_LANE_BLOCKS):
        ubuf[cur, lb, 0:CONV_HALO, :] = jnp.where(
            first_front, 0.0, ubuf[prev, lb, SEQ_TILE:SEQ_TILE + CONV_HALO, :])
    _rmsnorm_packed(lambda r: xf_ref[0, r, :], g_mix_ref, hp1_ref)
    for j in range(D_MODEL // MXU_N):
        cols = slice(j * MXU_N, (j + 1) * MXU_N)
        gcols = slice(D_MODEL + j * MXU_N, D_MODEL + (j + 1) * MXU_N)
        val = jnp.dot(_packed_as_bf16(hp1_ref[...]), w_pw1_ref[:, cols],
                      preferred_element_type=F32) + b_pw1_ref[:, cols]
        gate = jnp.dot(_packed_as_bf16(hp1_ref[...]), w_pw1_ref[:, gcols],
                       preferred_element_type=F32) + b_pw1_ref[:, gcols]
        u = val * jax.nn.sigmoid(gate)
        for lb in range(MXU_N // LANES):
            ubuf[cur, j * (MXU_N // LANES) + lb, CONV_HALO:, :] = u[:, _lanes(lb)]


def _pool_layer_kernel(
        tiles_per_seq,
        x_ref, g_mix_ref, w_pool_ref, b_pool_ref, scale_ref,
        g_ffn_ref, w_up_ref, w_dwf_ref, w_down_ref, g_final_ref,
        o_ref,
        pp_ref, hbuf, x2_ref, x1_ref, h2p_ref, ahalo, y_ref, *act_gate_bufs):
    act_bufs, gate_bufs = act_gate_bufs[:N_ACT_BUFS], act_gate_bufs[N_ACT_BUFS:]
    g = pl.program_id(0)
    seq_tile = g % tiles_per_seq
    first_front = seq_tile == 0
    first_back = (g + tiles_per_seq - 1) % tiles_per_seq == 0

    @pl.when(g == 0)
    def _():
        _zero(hbuf)
        _zero(x1_ref)
        _zero(h2p_ref)
        _zero(ahalo)

    def store_x2(cols, val):
        x2_ref[:, cols] = val
    _conv_ffn(first_back, h2p_ref, x1_ref, w_up_ref, w_dwf_ref, w_down_ref,
              act_bufs, gate_bufs, ahalo, y_ref, store_x2)
    for r in range(SEQ_TILE // PACKED_ROWS):
        rows = slice(r * PACKED_ROWS, (r + 1) * PACKED_ROWS)
        o_ref[0, rows, :] = _rmsnorm(x2_ref[rows, :], g_final_ref[...])

    for lb in range(N_LANE_BLOCKS):
        hbuf[lb, 0:POOL_HALO, :] = jnp.where(
            first_front, 0.0, hbuf[lb, SEQ_TILE:SEQ_TILE + POOL_HALO, :])
    for r in range(SEQ_TILE // PACKED_ROWS):
        rows = slice(r * PACKED_ROWS, (r + 1) * PACKED_ROWS)
        h = _rmsnorm(x_ref[0, rows, :], g_mix_ref[...])
        for lb in range(N_LANE_BLOCKS):
            hbuf[lb, POOL_HALO + r * PACKED_ROWS:POOL_HALO + (r + 1) * PACKED_ROWS, :] = h[:, _lanes(lb)]

    lanes_per_group = POOL_GROUP_DIM // LANES
    row_in_tile = lax.broadcasted_iota(jnp.int32, (PACKED_ROWS, LANES), 0)
    for r in range(SEQ_TILE // PACKED_ROWS):
        t = seq_tile * SEQ_TILE + r * PACKED_ROWS + row_in_tile
        for lb in range(N_LANE_BLOCKS):
            win = POOL_WINDOWS[lb // lanes_per_group]
            base = POOL_HALO + r * PACKED_ROWS
            cur = hbuf[lb, base:base + PACKED_ROWS, :]
            tot = cur
            for back in range(1, win):
                tot = tot + hbuf[lb, _rows(base - back, PACKED_ROWS), :]
            cnt = jnp.minimum(t + 1, win).astype(F32)
            pp_ref[r * SUBLANES:(r + 1) * SUBLANES, _lanes(lb)] = _pack_bf16(tot / cnt - cur)

    for grp in range(len(POOL_WINDOWS)):
        cols = slice(grp * POOL_GROUP_DIM, (grp + 1) * POOL_GROUP_DIM)
        mixed = jnp.dot(_packed_as_bf16(pp_ref[:, cols]), w_pool_ref[grp], preferred_element_type=F32)
        x1_ref[:, cols] = x_ref[0, :, cols] + scale_ref[:, cols] * (mixed + b_pool_ref[:, cols])
    _rmsnorm_packed(lambda r: x1_ref[r, :], g_ffn_ref, h2p_ref)


def _resident(shape):
    zeros = (0,) * len(shape)
    return pl.BlockSpec(shape, lambda g: zeros, pipeline_mode=pl.Buffered(1))


def _ffn_scratch():
    return [
        pltpu.VMEM((SEQ_TILE, D_MODEL), F32),
        pltpu.VMEM((SEQ_TILE // 2, D_MODEL), U32),
        pltpu.VMEM((N_FFN_CHUNKS, FFN_LANE_BLOCKS, FFN_HALO, LANES), F32),
        pltpu.VMEM((N_FFN_CHUNKS, SEQ_TILE // 2, FFN_CHUNK), U32),
    ] + [pltpu.VMEM((FFN_LANE_BLOCKS, FFN_HALO + SEQ_TILE, LANES), F32)] * N_ACT_BUFS \
      + [pltpu.VMEM((SEQ_TILE, FFN_CHUNK), F32)] * N_ACT_BUFS


def _lane_blocked_rows(rows):
    n, c = rows.shape
    pad = -n % SUBLANES
    rows = jnp.pad(rows, ((0, pad), (0, 0)))
    return rows.reshape(n + pad, c // LANES, LANES).transpose(1, 0, 2)


def _ffn_operands(w_up, w_dw, b_dw, w_down):
    taps = _lane_blocked_rows(jnp.concatenate([w_dw, b_dw[None, :]], axis=0))
    taps = taps.reshape(N_FFN_CHUNKS, FFN_LANE_BLOCKS, SUBLANES, LANES)
    return w_up.astype(BF16), taps, w_down.astype(BF16)


def _layer_call(kernel_fn, x, operands, scratch_shapes, name, skew):
    batch, seq, _ = x.shape
    tiles_per_seq = seq // SEQ_TILE
    n_tiles = batch * tiles_per_seq
    x_tiles = x.reshape(n_tiles, SEQ_TILE, D_MODEL)

    def tile_spec(lag):
        return pl.BlockSpec((1, SEQ_TILE, D_MODEL), lambda g: (jnp.clip(g - lag, 0, n_tiles - 1), 0, 0))
    out = pl.pallas_call(
        functools.partial(kernel_fn, tiles_per_seq),
        grid=(n_tiles + skew,),
        in_specs=[tile_spec(lag) for lag in range(skew)] + [_resident(op.shape) for op in operands],
        out_specs=tile_spec(skew),
        out_shape=jax.ShapeDtypeStruct(x_tiles.shape, x.dtype),
        scratch_shapes=scratch_shapes,
        compiler_params=pltpu.CompilerParams(
            dimension_semantics=("arbitrary",),
            vmem_limit_bytes=VMEM_LIMIT_BYTES),
        name=name,
    )(*([x_tiles] * skew), *operands)
    return out.reshape(x.shape)


def kernel(x, norm_mix, norm_ffn, conv_w_pw1, conv_b_pw1, conv_w_dw, conv_b_dw, conv_ln_g, conv_ln_b,
           conv_w_pw2, conv_b_pw2, pool_w, pool_b, pool_scale, ffn_w_up, ffn_w_dw, ffn_b_dw, ffn_w_down,
           final_norm):
    assert x.shape[1] % SEQ_TILE == 0 and x.shape[2] == D_MODEL
    row = lambda v: v.reshape(1, -1)

    conv_taps = _lane_blocked_rows(jnp.concatenate([conv_w_dw[0], conv_b_dw[0][None, :]], axis=0))
    conformer_operands = (
        jnp.zeros((SUBLANES, LANES), U32), row(norm_mix[0]), conv_w_pw1[0].astype(BF16), row(conv_b_pw1[0]), conv_taps,
        row(conv_ln_g[0]), row(conv_ln_b[0]), conv_w_pw2[0].astype(BF16), row(conv_b_pw2[0]),
        row(norm_ffn[0]), *_ffn_operands(ffn_w_up[0], ffn_w_dw[0], ffn_b_dw[0], ffn_w_down[0]))
    x = _layer_call(
        _conformer_layer_kernel, x, conformer_operands,
        [pltpu.VMEM((SEQ_TILE // 2, D_MODEL), U32),
         pltpu.VMEM((SEQ_TILE // 2, D_MODEL), U32),
         pltpu.VMEM((2, N_LANE_BLOCKS, CONV_HALO + SEQ_TILE, LANES), F32),
         pltpu.VMEM((SEQ_TILE, D_MODEL), F32),
         ] + _ffn_scratch(),
        "conformer_layer", skew=2)

    pool_operands = (
        row(norm_mix[1]), pool_w[0].astype(BF16), row(pool_b[0]), row(pool_scale[0]),
        row(norm_ffn[1]), *_ffn_operands(ffn_w_up[1], ffn_w_dw[1], ffn_b_dw[1], ffn_w_down[1]),
        row(final_norm))
    return _layer_call(
        _pool_layer_kernel, x, pool_operands,
        [pltpu.VMEM((SEQ_TILE // 2, D_MODEL), U32),
         pltpu.VMEM((N_LANE_BLOCKS, POOL_HALO + SEQ_TILE, LANES), F32),
         pltpu.VMEM((SEQ_TILE, D_MODEL), F32),
         ] + _ffn_scratch(),
        "pool_layer", skew=1)
```

```python
import functools

import jax
import jax.numpy as jnp
from jax import lax
from jax.experimental import pallas as pl
from jax.experimental.pallas import tpu as pltpu

F32 = jnp.float32
BF16 = jnp.bfloat16
U32 = jnp.uint32

D_MODEL = 1024
D_FF = 2816
CONV_KERNEL = 31
FFN_CONV_KERNEL = 3
POOL_WINDOWS = (2, 4, 8, 16)
POOL_GROUP_DIM = D_MODEL // len(POOL_WINDOWS)
RMS_EPS = 1e-6
LN_EPS = 1e-5

SUBLANES = 8
LANES = 128
PACKED_ROWS = 2 * SUBLANES
MXU_N = 256

SEQ_TILE = 512
FFN_CHUNK = MXU_N
N_FFN_CHUNKS = D_FF // FFN_CHUNK
FFN_LANE_BLOCKS = FFN_CHUNK // LANES
N_LANE_BLOCKS = D_MODEL // LANES
CONV_HALO = 32
FFN_HALO = SUBLANES
POOL_HALO = 16
CONV_ROWS = 32
CONV_CHAINS = 2
N_ACT_BUFS = 3
VMEM_LIMIT_BYTES = 58 * 1024 * 1024


def _rows(start, n):
    return pl.ds(start, n, stride=1)


def _bcast_row(k):
    return pl.ds(k, SUBLANES, stride=0)


def _lanes(lb):
    return slice(lb * LANES, (lb + 1) * LANES)


def _pack_bf16(x):
    return pltpu.bitcast(x.astype(BF16), U32)


def _packed_as_bf16(ref_or_val):
    return pltpu.bitcast(ref_or_val, BF16)


def _rmsnorm(x, g):
    ms = jnp.mean(x * x, axis=-1, keepdims=True)
    return (x * lax.rsqrt(ms + RMS_EPS)) * g


def _after(x, token, zero_bits):
    dep = pltpu.bitcast(pltpu.roll(token, 1, axis=1), U32) & zero_bits
    return pltpu.bitcast(pltpu.bitcast(x, U32) | dep, F32)


def _rmsnorm_packed(load_rows, g_ref, dst_ref):
    for r in range(SEQ_TILE // PACKED_ROWS):
        x = load_rows(slice(r * PACKED_ROWS, (r + 1) * PACKED_ROWS))
        dst_ref[r * SUBLANES:(r + 1) * SUBLANES, :] = _pack_bf16(_rmsnorm(x, g_ref[...]))


def _ffn_up(first_tile, h2p_ref, w_up_ref, w_dw_ref, act_bufs, gate_bufs, ahalo, y_ref,
            waits=(), zero_bits=None):
    waits = list(waits)
    per_chunk = -(-len(waits) // N_FFN_CHUNKS)
    for c in range(N_FFN_CHUNKS):
        chunk_waits = waits[c * per_chunk:(c + 1) * per_chunk]
        abuf = act_bufs[c % N_ACT_BUFS]
        gbuf = gate_bufs[c % N_ACT_BUFS]
        chunk = slice(c * FFN_CHUNK, (c + 1) * FFN_CHUNK)
        gate_chunk = slice(D_FF + c * FFN_CHUNK, D_FF + (c + 1) * FFN_CHUNK)
        act = jnp.dot(_packed_as_bf16(h2p_ref[...]), w_up_ref[:, chunk], preferred_element_type=F32)
        gbuf[...] = jnp.dot(_packed_as_bf16(h2p_ref[...]), w_up_ref[:, gate_chunk],
                            preferred_element_type=F32)
        for lb in range(FFN_LANE_BLOCKS):
            abuf[lb, 0:FFN_HALO, :] = jnp.where(first_tile, 0.0, ahalo[c, lb])
            abuf[lb, FFN_HALO:, :] = act[:, _lanes(lb)]
            ahalo[c, lb] = abuf[lb, SEQ_TILE:SEQ_TILE + FFN_HALO, :]
            taps = [w_dw_ref[c, lb, _bcast_row(k), :] for k in range(FFN_CONV_KERNEL)]
            bias = w_dw_ref[c, lb, _bcast_row(FFN_CONV_KERNEL), :]
            for r in range(SEQ_TILE // PACKED_ROWS):
                halves = []
                blk_bias = bias
                if chunk_waits and lb == 0:
                    blk_bias = _after(bias, chunk_waits.pop(), zero_bits)
                for half in range(2):
                    r0 = r * PACKED_ROWS + half * SUBLANES
                    a = blk_bias
                    for k in range(FFN_CONV_KERNEL):
                        start = r0 + FFN_HALO - (FFN_CONV_KERNEL - 1) + k
                        a = a + taps[k] * abuf[lb, _rows(start, SUBLANES), :]
                    halves.append(jax.nn.silu(a) * gbuf[r0:r0 + SUBLANES, _lanes(lb)])
                y_ref[c, r * SUBLANES:(r + 1) * SUBLANES, _lanes(lb)] = _pack_bf16(
                    jnp.concatenate(halves, axis=0))


def _ffn_down(res_ref, w_down_ref, y_ref, store_cols):
    for j in range(D_MODEL // MXU_N):
        cols = slice(j * MXU_N, (j + 1) * MXU_N)
        acc = res_ref[:, cols]
        for c in range(N_FFN_CHUNKS):
            acc = acc + jnp.dot(_packed_as_bf16(y_ref[c]),
                                w_down_ref[c * FFN_CHUNK:(c + 1) * FFN_CHUNK, cols],
                                preferred_element_type=F32)
        store_cols(cols, acc)


def _zero(ref):
    ref[...] = jnp.zeros(ref.shape, ref.dtype)


def _conformer_layer_kernel(
        tiles_per_seq,
        xf_ref, xm_ref, zero_ref, g_mix_ref, w_pw1_ref, b_pw1_ref, w_dw_ref, ln_g_ref, ln_b_ref,
        w_pw2_ref, b_pw2_ref,
        g_ffn_ref, w_up_ref, w_dwf_ref, w_down_ref,
        o_ref,
        hp1_ref, hpv_ref, ubuf, cv_ref, x1_ref, h2p_ref, ahalo, y_ref, *act_gate_bufs):
    act_bufs, gate_bufs = act_gate_bufs[:N_ACT_BUFS], act_gate_bufs[N_ACT_BUFS:]
    g = pl.program_id(0)
    first_front = g % tiles_per_seq == 0
    first_back = (g + 2 * tiles_per_seq - 2) % tiles_per_seq == 0
    cur = g % 2
    prev = 1 - cur

    @pl.when(g == 0)
    def _():
        _zero(ubuf)
        _zero(x1_ref)
        _zero(h2p_ref)
        _zero(ahalo)

    def store_out(cols, val):
        o_ref[0, :, cols] = val
    _ffn_up(first_back, h2p_ref, w_up_ref, w_dwf_ref, act_bufs, gate_bufs, ahalo, y_ref)
    _ffn_down(x1_ref, w_down_ref, y_ref, store_out)

    tokens = []
    for rb in range(SEQ_TILE // CONV_ROWS):
        for lb in range(N_LANE_BLOCKS):
            base = rb * CONV_ROWS
            n_sub = CONV_ROWS // SUBLANES
            bias = w_dw_ref[lb, _bcast_row(CONV_KERNEL), :]
            if len(tokens) >= CONV_CHAINS:
                bias = _after(bias, tokens[-CONV_CHAINS], zero_ref[...])
            accs = [bias] * n_sub
            for k in range(CONV_KERNEL):
                tap = w_dw_ref[lb, _bcast_row(k), :]
                for s in range(n_sub):
                    start = base + s * SUBLANES + CONV_HALO - (CONV_KERNEL - 1) + k
                    accs[s] = accs[s] + tap * ubuf[prev, lb, _rows(start, SUBLANES), :]
            token = accs[0]
            for s in range(n_sub):
                r0 = base + s * SUBLANES
                cv_ref[r0:r0 + SUBLANES, _lanes(lb)] = accs[s]
                if s:
                    token = jnp.maximum(token, accs[s])
            tokens.append(token)

    for r in range(SEQ_TILE // PACKED_ROWS):
        v = cv_ref[r * PACKED_ROWS:(r + 1) * PACKED_ROWS, :]
        mu = jnp.mean(v, axis=-1, keepdims=True)
        var = jnp.mean(jnp.square(v - mu), axis=-1, keepdims=True)
        y = (v - mu) * lax.rsqrt(var + LN_EPS)
        hpv_ref[r * SUBLANES:(r + 1) * SUBLANES, :] = _pack_bf16(
            jax.nn.silu(y * ln_g_ref[...] + ln_b_ref[...]))

    for j in range(D_MODEL // MXU_N):
        cols = slice(j * MXU_N, (j + 1) * MXU_N)
        x1_ref[:, cols] = (xm_ref[0, :, cols] + b_pw2_ref[:, cols]
                           + jnp.dot(_packed_as_bf16(hpv_ref[...]), w_pw2_ref[:, cols],
                                     preferred_element_type=F32))
    _rmsnorm_packed(lambda r: x1_ref[r, :], g_ffn_ref, h2p_ref)

    for lb in range(N_LANE_BLOCKS):
        ubuf[cur, lb, 0:CONV_HALO, :] = jnp.where(
            first_front, 0.0, ubuf[prev, lb, SEQ_TILE:SEQ_TILE + CONV_HALO, :])
    _rmsnorm_packed(lambda r: xf_ref[0, r, :], g_mix_ref, hp1_ref)
    for j in range(D_MODEL // MXU_N):
        cols = slice(j * MXU_N, (j + 1) * MXU_N)
        gcols = slice(D_MODEL + j * MXU_N, D_MODEL + (j + 1) * MXU_N)
        val = jnp.dot(_packed_as_bf16(hp1_ref[...]), w_pw1_ref[:, cols],
                      preferred_element_type=F32) + b_pw1_ref[:, cols]
        gate = jnp.dot(_packed_as_bf16(hp1_ref[...]), w_pw1_ref[:, gcols],
                       preferred_element_type=F32) + b_pw1_ref[:, gcols]
        u = val * jax.nn.sigmoid(gate)
        for lb in range(MXU_N // LANES):
            ubuf[cur, j * (MXU_N // LANES) + lb, CONV_HALO:, :] = u[:, _lanes(lb)]


def _pool_layer_kernel(
        tiles_per_seq,
        x_ref, zero_ref, g_mix_ref, w_pool_ref, b_pool_ref, scale_ref,
        g_ffn_ref, w_up_ref, w_dwf_ref, w_down_ref, g_final_ref,
        o_ref,
        pp_ref, hbuf, x2_ref, x1_ref, h2p_ref, ahalo, y_ref, *act_gate_bufs):
    act_bufs, gate_bufs = act_gate_bufs[:N_ACT_BUFS], act_gate_bufs[N_ACT_BUFS:]
    g = pl.program_id(0)
    seq_tile = g % tiles_per_seq
    first_front = seq_tile == 0
    first_back = (g + tiles_per_seq - 1) % tiles_per_seq == 0

    @pl.when(g == 0)
    def _():
        _zero(hbuf)
        _zero(x1_ref)
        _zero(h2p_ref)
        _zero(ahalo)

    waits = []
    for lb in range(N_LANE_BLOCKS):
        hbuf[lb, 0:POOL_HALO, :] = jnp.where(
            first_front, 0.0, hbuf[lb, SEQ_TILE:SEQ_TILE + POOL_HALO, :])
    for r in range(SEQ_TILE // PACKED_ROWS):
        rows = slice(r * PACKED_ROWS, (r + 1) * PACKED_ROWS)
        h = _rmsnorm(x_ref[0, rows, :], g_mix_ref[...])
        for lb in range(N_LANE_BLOCKS):
            hbuf[lb, POOL_HALO + r * PACKED_ROWS:POOL_HALO + (r + 1) * PACKED_ROWS, :] = h[:, _lanes(lb)]
        waits.append(h[0:SUBLANES, 0:LANES])

    lanes_per_group = POOL_GROUP_DIM // LANES
    row_in_tile = lax.broadcasted_iota(jnp.int32, (PACKED_ROWS, LANES), 0)
    for r in range(SEQ_TILE // PACKED_ROWS):
        t = seq_tile * SEQ_TILE + r * PACKED_ROWS + row_in_tile
        for lb in range(N_LANE_BLOCKS):
            win = POOL_WINDOWS[lb // lanes_per_group]
            base = POOL_HALO + r * PACKED_ROWS
            cur = hbuf[lb, base:base + PACKED_ROWS, :]
            tot = cur
            for back in range(1, win):
                tot = tot + hbuf[lb, _rows(base - back, PACKED_ROWS), :]
            cnt = jnp.minimum(t + 1, win).astype(F32)
            pooled = _pack_bf16(tot / cnt - cur)
            pp_ref[r * SUBLANES:(r + 1) * SUBLANES, _lanes(lb)] = pooled
            waits.append(pltpu.bitcast(pooled, F32))

    def store_x2(cols, val):
        x2_ref[:, cols] = val
    _ffn_up(first_back, h2p_ref, w_up_ref, w_dwf_ref, act_bufs, gate_bufs, ahalo, y_ref,
            waits=waits, zero_bits=zero_ref[...])
    _ffn_down(x1_ref, w_down_ref, y_ref, store_x2)
    for r in range(SEQ_TILE // PACKED_ROWS):
        rows = slice(r * PACKED_ROWS, (r + 1) * PACKED_ROWS)
        o_ref[0, rows, :] = _rmsnorm(x2_ref[rows, :], g_final_ref[...])

    for grp in range(len(POOL_WINDOWS)):
        cols = slice(grp * POOL_GROUP_DIM, (grp + 1) * POOL_GROUP_DIM)
        mixed = jnp.dot(_packed_as_bf16(pp_ref[:, cols]), w_pool_ref[grp], preferred_element_type=F32)
        x1_ref[:, cols] = x_ref[0, :, cols] + scale_ref[:, cols] * (mixed + b_pool_ref[:, cols])
    _rmsnorm_packed(lambda r: x1_ref[r, :], g_ffn_ref, h2p_ref)


def _resident(array):
    zeros = (0,) * array.ndim
    return array, pl.BlockSpec(array.shape, lambda g: zeros, pipeline_mode=pl.Buffered(1))


def _resident_layer(stacked, layer):
    index = (layer,) + (0,) * (stacked.ndim - 1)
    return stacked, pl.BlockSpec((None,) + stacked.shape[1:], lambda g: index,
                                 pipeline_mode=pl.Buffered(1))


def _ffn_scratch():
    return [
        pltpu.VMEM((SEQ_TILE, D_MODEL), F32),
        pltpu.VMEM((SEQ_TILE // 2, D_MODEL), U32),
        pltpu.VMEM((N_FFN_CHUNKS, FFN_LANE_BLOCKS, FFN_HALO, LANES), F32),
        pltpu.VMEM((N_FFN_CHUNKS, SEQ_TILE // 2, FFN_CHUNK), U32),
    ] + [pltpu.VMEM((FFN_LANE_BLOCKS, FFN_HALO + SEQ_TILE, LANES), F32)] * N_ACT_BUFS \
      + [pltpu.VMEM((SEQ_TILE, FFN_CHUNK), F32)] * N_ACT_BUFS


def _lane_blocked_rows(rows):
    n, c = rows.shape
    pad = -n % SUBLANES
    rows = jnp.pad(rows, ((0, pad), (0, 0)))
    return rows.reshape(n + pad, c // LANES, LANES).transpose(1, 0, 2)


def _ffn_operands(layer, w_up_bf16, w_dw, b_dw, w_down_bf16):
    taps = _lane_blocked_rows(jnp.concatenate([w_dw, b_dw[None, :]], axis=0))
    taps = taps.reshape(N_FFN_CHUNKS, FFN_LANE_BLOCKS, SUBLANES, LANES)
    return (_resident_layer(w_up_bf16, layer), _resident(taps), _resident_layer(w_down_bf16, layer))


def _layer_call(kernel_fn, x, operands, scratch_shapes, name, n_inputs, skew):
    batch, seq, _ = x.shape
    tiles_per_seq = seq // SEQ_TILE
    n_tiles = batch * tiles_per_seq
    x_tiles = x.reshape(n_tiles, SEQ_TILE, D_MODEL)

    def tile_spec(lag):
        return pl.BlockSpec((1, SEQ_TILE, D_MODEL), lambda g: (jnp.clip(g - lag, 0, n_tiles - 1), 0, 0))
    out = pl.pallas_call(
        functools.partial(kernel_fn, tiles_per_seq),
        grid=(n_tiles + skew,),
        in_specs=[tile_spec(lag) for lag in range(n_inputs)] + [spec for _, spec in operands],
        out_specs=tile_spec(skew),
        out_shape=jax.ShapeDtypeStruct(x_tiles.shape, x.dtype),
        scratch_shapes=scratch_shapes,
        compiler_params=pltpu.CompilerParams(
            dimension_semantics=("arbitrary",),
            vmem_limit_bytes=VMEM_LIMIT_BYTES),
        name=name,
    )(*([x_tiles] * n_inputs), *[op for op, _ in operands])
    return out.reshape(x.shape)


def kernel(x, norm_mix, norm_ffn, conv_w_pw1, conv_b_pw1, conv_w_dw, conv_b_dw, conv_ln_g, conv_ln_b,
           conv_w_pw2, conv_b_pw2, pool_w, pool_b, pool_scale, ffn_w_up, ffn_w_dw, ffn_b_dw, ffn_w_down,
           final_norm):
    assert x.shape[1] % SEQ_TILE == 0 and x.shape[2] == D_MODEL
    row = lambda v: _resident(v.reshape(1, -1))
    zero_bits = _resident(jnp.zeros((SUBLANES, LANES), U32))
    w_up_bf16 = ffn_w_up.astype(BF16)
    w_down_bf16 = ffn_w_down.astype(BF16)

    conv_taps = _lane_blocked_rows(jnp.concatenate([conv_w_dw[0], conv_b_dw[0][None, :]], axis=0))
    conformer_operands = (
        zero_bits, row(norm_mix[0]), _resident(conv_w_pw1[0].astype(BF16)), row(conv_b_pw1[0]),
        _resident(conv_taps), row(conv_ln_g[0]), row(conv_ln_b[0]),
        _resident(conv_w_pw2[0].astype(BF16)), row(conv_b_pw2[0]),
        row(norm_ffn[0]), *_ffn_operands(0, w_up_bf16, ffn_w_dw[0], ffn_b_dw[0], w_down_bf16))
    x = _layer_call(
        _conformer_layer_kernel, x, conformer_operands,
        [pltpu.VMEM((SEQ_TILE // 2, D_MODEL), U32),
         pltpu.VMEM((SEQ_TILE // 2, D_MODEL), U32),
         pltpu.VMEM((2, N_LANE_BLOCKS, CONV_HALO + SEQ_TILE, LANES), F32),
         pltpu.VMEM((SEQ_TILE, D_MODEL), F32),
         ] + _ffn_scratch(),
        "conformer_layer", n_inputs=2, skew=2)

    pool_operands = (
        zero_bits, row(norm_mix[1]), _resident(pool_w[0].astype(BF16)), row(pool_b[0]), row(pool_scale[0]),
        row(norm_ffn[1]), *_ffn_operands(1, w_up_bf16, ffn_w_dw[1], ffn_b_dw[1], w_down_bf16),
        row(final_norm))
    return _layer_call(
        _pool_layer_kernel, x, pool_operands,
        [pltpu.VMEM((SEQ_TILE // 2, D_MODEL), U32),
         pltpu.VMEM((N_LANE_BLOCKS, POOL_HALO + SEQ_TILE, LANES), F32),
         pltpu.VMEM((SEQ_TILE, D_MODEL), F32),
         ] + _ffn_scratch(),
        "pool_layer", n_inputs=1, skew=1)
```

```python
import functools

import jax
import jax.numpy as jnp
from jax import lax
from jax.experimental import pallas as pl
from jax.experimental.pallas import tpu as pltpu

F32 = jnp.float32
BF16 = jnp.bfloat16
U32 = jnp.uint32

D_MODEL = 1024
D_FF = 2816
CONV_KERNEL = 31
FFN_CONV_KERNEL = 3
POOL_WINDOWS = (2, 4, 8, 16)
POOL_GROUP_DIM = D_MODEL // len(POOL_WINDOWS)
RMS_EPS = 1e-6
LN_EPS = 1e-5

SUBLANES = 8
LANES = 128
PACKED_ROWS = 2 * SUBLANES
MXU_N = 256

SEQ_TILE = 512
FFN_CHUNK = MXU_N
N_FFN_CHUNKS = D_FF // FFN_CHUNK
FFN_LANE_BLOCKS = FFN_CHUNK // LANES
N_LANE_BLOCKS = D_MODEL // LANES
CONV_HALO = 32
FFN_HALO = SUBLANES
POOL_HALO = 16
CONV_ROWS = 32
CONV_CHAINS = 2
N_ACT_BUFS = 6
VMEM_LIMIT_BYTES = 58 * 1024 * 1024


def _rows(start, n):
    return pl.ds(start, n, stride=1)


def _bcast_row(k):
    return pl.ds(k, SUBLANES, stride=0)


def _lanes(lb):
    return slice(lb * LANES, (lb + 1) * LANES)


def _pack_bf16(x):
    return pltpu.bitcast(x.astype(BF16), U32)


def _packed_as_bf16(ref_or_val):
    return pltpu.bitcast(ref_or_val, BF16)


def _rmsnorm(x, g):
    ms = jnp.mean(x * x, axis=-1, keepdims=True)
    return (x * lax.rsqrt(ms + RMS_EPS)) * g


def _after(x, token, zero_bits):
    dep = pltpu.bitcast(pltpu.roll(token, 1, axis=1), U32) & zero_bits
    return pltpu.bitcast(pltpu.bitcast(x, U32) | dep, F32)


def _rmsnorm_packed(load_rows, g_ref, dst_ref):
    for r in range(SEQ_TILE // PACKED_ROWS):
        x = load_rows(slice(r * PACKED_ROWS, (r + 1) * PACKED_ROWS))
        dst_ref[r * SUBLANES:(r + 1) * SUBLANES, :] = _pack_bf16(_rmsnorm(x, g_ref[...]))


def _ffn_up(first_tile, h2p_ref, w_up_ref, w_dw_ref, act_bufs, gate_bufs, ahalo, y_ref,
            waits=(), zero_bits=None):
    waits = list(waits)
    per_chunk = -(-len(waits) // N_FFN_CHUNKS)
    for c in range(N_FFN_CHUNKS):
        chunk_waits = waits[c * per_chunk:(c + 1) * per_chunk]
        abuf = act_bufs[c % N_ACT_BUFS]
        gbuf = gate_bufs[c % N_ACT_BUFS]
        chunk = slice(c * FFN_CHUNK, (c + 1) * FFN_CHUNK)
        gate_chunk = slice(D_FF + c * FFN_CHUNK, D_FF + (c + 1) * FFN_CHUNK)
        act = jnp.dot(_packed_as_bf16(h2p_ref[...]), w_up_ref[:, chunk], preferred_element_type=F32)
        gbuf[...] = jnp.dot(_packed_as_bf16(h2p_ref[...]), w_up_ref[:, gate_chunk],
                            preferred_element_type=F32)
        for lb in range(FFN_LANE_BLOCKS):
            abuf[lb, 0:FFN_HALO, :] = jnp.where(first_tile, 0.0, ahalo[c, lb])
            abuf[lb, FFN_HALO:, :] = act[:, _lanes(lb)]
            ahalo[c, lb] = abuf[lb, SEQ_TILE:SEQ_TILE + FFN_HALO, :]
            taps = [w_dw_ref[c, lb, _bcast_row(k), :] for k in range(FFN_CONV_KERNEL)]
            bias = w_dw_ref[c, lb, _bcast_row(FFN_CONV_KERNEL), :]
            for r in range(SEQ_TILE // PACKED_ROWS):
                halves = []
                blk_bias = bias
                if chunk_waits and lb == 0:
                    blk_bias = _after(bias, chunk_waits.pop(), zero_bits)
                for half in range(2):
                    r0 = r * PACKED_ROWS + half * SUBLANES
                    a = blk_bias
                    for k in range(FFN_CONV_KERNEL):
                        start = r0 + FFN_HALO - (FFN_CONV_KERNEL - 1) + k
                        a = a + taps[k] * abuf[lb, _rows(start, SUBLANES), :]
                    halves.append(jax.nn.silu(a) * gbuf[r0:r0 + SUBLANES, _lanes(lb)])
                y_ref[c, r * SUBLANES:(r + 1) * SUBLANES, _lanes(lb)] = _pack_bf16(
                    jnp.concatenate(halves, axis=0))


def _ffn_down(res_ref, w_down_ref, y_ref, store_cols):
    for j in range(D_MODEL // MXU_N):
        cols = slice(j * MXU_N, (j + 1) * MXU_N)
        acc = res_ref[:, cols]
        for c in range(N_FFN_CHUNKS):
            acc = acc + jnp.dot(_packed_as_bf16(y_ref[c]),
                                w_down_ref[c * FFN_CHUNK:(c + 1) * FFN_CHUNK, cols],
                                preferred_element_type=F32)
        store_cols(cols, acc)


def _zero(ref):
    ref[...] = jnp.zeros(ref.shape, ref.dtype)


def _conformer_layer_kernel(
        tiles_per_seq,
        xf_ref, xm_ref, zero_ref, g_mix_ref, w_pw1_ref, b_pw1_ref, w_dw_ref, ln_g_ref, ln_b_ref,
        w_pw2_ref, b_pw2_ref,
        g_ffn_ref, w_up_ref, w_dwf_ref, w_down_ref,
        o_ref,
        hp1_ref, hpv_ref, ubuf, cv_ref, x1_ref, h2p_ref, ahalo, y_ref, *act_gate_bufs):
    act_bufs, gate_bufs = act_gate_bufs[:N_ACT_BUFS], act_gate_bufs[N_ACT_BUFS:]
    g = pl.program_id(0)
    first_front = g % tiles_per_seq == 0
    first_back = (g + 2 * tiles_per_seq - 2) % tiles_per_seq == 0
    cur = g % 2
    prev = 1 - cur

    @pl.when(g == 0)
    def _():
        _zero(ubuf)
        _zero(x1_ref)
        _zero(h2p_ref)
        _zero(ahalo)

    def store_out(cols, val):
        o_ref[0, :, cols] = val
    _ffn_up(first_back, h2p_ref, w_up_ref, w_dwf_ref, act_bufs, gate_bufs, ahalo, y_ref)
    _ffn_down(x1_ref, w_down_ref, y_ref, store_out)

    tokens = []
    for rb in range(SEQ_TILE // CONV_ROWS):
        for lb in range(N_LANE_BLOCKS):
            base = rb * CONV_ROWS
            n_sub = CONV_ROWS // SUBLANES
            bias = w_dw_ref[lb, _bcast_row(CONV_KERNEL), :]
            if len(tokens) >= CONV_CHAINS:
                bias = _after(bias, tokens[-CONV_CHAINS], zero_ref[...])
            accs = [bias] * n_sub
            for k in range(CONV_KERNEL):
                tap = w_dw_ref[lb, _bcast_row(k), :]
                for s in range(n_sub):
                    start = base + s * SUBLANES + CONV_HALO - (CONV_KERNEL - 1) + k
                    accs[s] = accs[s] + tap * ubuf[prev, lb, _rows(start, SUBLANES), :]
            token = accs[0]
            for s in range(n_sub):
                r0 = base + s * SUBLANES
                cv_ref[r0:r0 + SUBLANES, _lanes(lb)] = accs[s]
                if s:
                    token = jnp.maximum(token, accs[s])
            tokens.append(token)

    for r in range(SEQ_TILE // PACKED_ROWS):
        v = cv_ref[r * PACKED_ROWS:(r + 1) * PACKED_ROWS, :]
        mu = jnp.mean(v, axis=-1, keepdims=True)
        var = jnp.mean(jnp.square(v - mu), axis=-1, keepdims=True)
        y = (v - mu) * lax.rsqrt(var + LN_EPS)
        hpv_ref[r * SUBLANES:(r + 1) * SUBLANES, :] = _pack_bf16(
            jax.nn.silu(y * ln_g_ref[...] + ln_b_ref[...]))

    for j in range(D_MODEL // MXU_N):
        cols = slice(j * MXU_N, (j + 1) * MXU_N)
        x1_ref[:, cols] = (xm_ref[0, :, cols] + b_pw2_ref[:, cols]
                           + jnp.dot(_packed_as_bf16(hpv_ref[...]), w_pw2_ref[:, cols],
                                     preferred_element_type=F32))
    _rmsnorm_packed(lambda r: x1_ref[r, :], g_ffn_ref, h2p_ref)

    for lb in range(N_LANE_BLOCKS):
        ubuf[cur, lb, 0:CONV_HALO, :] = jnp.where(
            first_front, 0.0, ubuf[prev, lb, SEQ_TILE:SEQ_TILE + CONV_HALO, :])
    _rmsnorm_packed(lambda r: xf_ref[0, r, :], g_mix_ref, hp1_ref)
    for j in range(D_MODEL // MXU_N):
        cols = slice(j * MXU_N, (j + 1) * MXU_N)
        gcols = slice(D_MODEL + j * MXU_N, D_MODEL + (j + 1) * MXU_N)
        val = jnp.dot(_packed_as_bf16(hp1_ref[...]), w_pw1_ref[:, cols],
                      preferred_element_type=F32) + b_pw1_ref[:, cols]
        gate = jnp.dot(_packed_as_bf16(hp1_ref[...]), w_pw1_ref[:, gcols],
                       preferred_element_type=F32) + b_pw1_ref[:, gcols]
        u = val * jax.nn.sigmoid(gate)
        for lb in range(MXU_N // LANES):
            ubuf[cur, j * (MXU_N // LANES) + lb, CONV_HALO:, :] = u[:, _lanes(lb)]


def _pool_layer_kernel(
        tiles_per_seq,
        x_ref, zero_ref, g_mix_ref, w_pool_ref, b_pool_ref, scale_ref,
        g_ffn_ref, w_up_ref, w_dwf_ref, w_down_ref, g_final_ref,
        o_ref,
        pp_ref, hbuf, x2_ref, x1_ref, h2p_ref, ahalo, y_ref, *act_gate_bufs):
    act_bufs, gate_bufs = act_gate_bufs[:N_ACT_BUFS], act_gate_bufs[N_ACT_BUFS:]
    g = pl.program_id(0)
    seq_tile = g % tiles_per_seq
    first_front = seq_tile == 0
    first_back = (g + tiles_per_seq - 1) % tiles_per_seq == 0

    @pl.when(g == 0)
    def _():
        _zero(hbuf)
        _zero(x1_ref)
        _zero(h2p_ref)
        _zero(ahalo)

    waits = []
    for lb in range(N_LANE_BLOCKS):
        hbuf[lb, 0:POOL_HALO, :] = jnp.where(
            first_front, 0.0, hbuf[lb, SEQ_TILE:SEQ_TILE + POOL_HALO, :])
    for r in range(SEQ_TILE // PACKED_ROWS):
        rows = slice(r * PACKED_ROWS, (r + 1) * PACKED_ROWS)
        h = _rmsnorm(x_ref[0, rows, :], g_mix_ref[...])
        for lb in range(N_LANE_BLOCKS):
            hbuf[lb, POOL_HALO + r * PACKED_ROWS:POOL_HALO + (r + 1) * PACKED_ROWS, :] = h[:, _lanes(lb)]
        waits.append(h[0:SUBLANES, 0:LANES])

    lanes_per_group = POOL_GROUP_DIM // LANES
    row_in_tile = lax.broadcasted_iota(jnp.int32, (PACKED_ROWS, LANES), 0)
    for r in range(SEQ_TILE // PACKED_ROWS):
        t = seq_tile * SEQ_TILE + r * PACKED_ROWS + row_in_tile
        for lb in range(N_LANE_BLOCKS):
            win = POOL_WINDOWS[lb // lanes_per_group]
            base = POOL_HALO + r * PACKED_ROWS
            cur = hbuf[lb, base:base + PACKED_ROWS, :]
            tot = cur
            for back in range(1, win):
                tot = tot + hbuf[lb, _rows(base - back, PACKED_ROWS), :]
            cnt = jnp.minimum(t + 1, win).astype(F32)
            pooled = _pack_bf16(tot / cnt - cur)
            pp_ref[r * SUBLANES:(r + 1) * SUBLANES, _lanes(lb)] = pooled
            waits.append(pltpu.bitcast(pooled, F32))

    def store_x2(cols, val):
        x2_ref[:, cols] = val
    _ffn_up(first_back, h2p_ref, w_up_ref, w_dwf_ref, act_bufs, gate_bufs, ahalo, y_ref,
            waits=waits, zero_bits=zero_ref[...])
    _ffn_down(x1_ref, w_down_ref, y_ref, store_x2)
    for r in range(SEQ_TILE // PACKED_ROWS):
        rows = slice(r * PACKED_ROWS, (r + 1) * PACKED_ROWS)
        o_ref[0, rows, :] = _rmsnorm(x2_ref[rows, :], g_final_ref[...])

    for grp in range(len(POOL_WINDOWS)):
        cols = slice(grp * POOL_GROUP_DIM, (grp + 1) * POOL_GROUP_DIM)
        mixed = jnp.dot(_packed_as_bf16(pp_ref[:, cols]), w_pool_ref[grp], preferred_element_type=F32)
        x1_ref[:, cols] = x_ref[0, :, cols] + scale_ref[:, cols] * (mixed + b_pool_ref[:, cols])
    _rmsnorm_packed(lambda r: x1_ref[r, :], g_ffn_ref, h2p_ref)


def _resident(array):
    zeros = (0,) * array.ndim
    return array, pl.BlockSpec(array.shape, lambda g: zeros, pipeline_mode=pl.Buffered(1))


def _resident_layer(stacked, layer):
    index = (layer,) + (0,) * (stacked.ndim - 1)
    return stacked, pl.BlockSpec((None,) + stacked.shape[1:], lambda g: index,
                                 pipeline_mode=pl.Buffered(1))


def _ffn_scratch():
    return [
        pltpu.VMEM((SEQ_TILE, D_MODEL), F32),
        pltpu.VMEM((SEQ_TILE // 2, D_MODEL), U32),
        pltpu.VMEM((N_FFN_CHUNKS, FFN_LANE_BLOCKS, FFN_HALO, LANES), F32),
        pltpu.VMEM((N_FFN_CHUNKS, SEQ_TILE // 2, FFN_CHUNK), U32),
    ] + [pltpu.VMEM((FFN_LANE_BLOCKS, FFN_HALO + SEQ_TILE, LANES), F32)] * N_ACT_BUFS \
      + [pltpu.VMEM((SEQ_TILE, FFN_CHUNK), F32)] * N_ACT_BUFS


def _lane_blocked_rows(rows):
    n, c = rows.shape
    pad = -n % SUBLANES
    rows = jnp.pad(rows, ((0, pad), (0, 0)))
    return rows.reshape(n + pad, c // LANES, LANES).transpose(1, 0, 2)


def _ffn_operands(layer, w_up_bf16, w_dw, b_dw, w_down_bf16):
    taps = _lane_blocked_rows(jnp.concatenate([w_dw, b_dw[None, :]], axis=0))
    taps = taps.reshape(N_FFN_CHUNKS, FFN_LANE_BLOCKS, SUBLANES, LANES)
    return (_resident_layer(w_up_bf16, layer), _resident(taps), _resident_layer(w_down_bf16, layer))


def _layer_call(kernel_fn, x, operands, scratch_shapes, name, n_inputs, skew):
    batch, seq, _ = x.shape
    tiles_per_seq = seq // SEQ_TILE
    n_tiles = batch * tiles_per_seq
    x_tiles = x.reshape(n_tiles, SEQ_TILE, D_MODEL)

    def tile_spec(lag):
        return pl.BlockSpec((1, SEQ_TILE, D_MODEL), lambda g: (jnp.clip(g - lag, 0, n_tiles - 1), 0, 0))
    out = pl.pallas_call(
        functools.partial(kernel_fn, tiles_per_seq),
        grid=(n_tiles + skew,),
        in_specs=[tile_spec(lag) for lag in range(n_inputs)] + [spec for _, spec in operands],
        out_specs=tile_spec(skew),
        out_shape=jax.ShapeDtypeStruct(x_tiles.shape, x.dtype),
        scratch_shapes=scratch_shapes,
        compiler_params=pltpu.CompilerParams(
            dimension_semantics=("arbitrary",),
            vmem_limit_bytes=VMEM_LIMIT_BYTES),
        name=name,
    )(*([x_tiles] * n_inputs), *[op for op, _ in operands])
    return out.reshape(x.shape)


def kernel(x, norm_mix, norm_ffn, conv_w_pw1, conv_b_pw1, conv_w_dw, conv_b_dw, conv_ln_g, conv_ln_b,
           conv_w_pw2, conv_b_pw2, pool_w, pool_b, pool_scale, ffn_w_up, ffn_w_dw, ffn_b_dw, ffn_w_down,
           final_norm):
    assert x.shape[1] % SEQ_TILE == 0 and x.shape[2] == D_MODEL
    row = lambda v: _resident(v.reshape(1, -1))
    zero_bits = _resident(jnp.zeros((SUBLANES, LANES), U32))
    w_up_bf16 = ffn_w_up.astype(BF16)
    w_down_bf16 = ffn_w_down.astype(BF16)

    conv_taps = _lane_blocked_rows(jnp.concatenate([conv_w_dw[0], conv_b_dw[0][None, :]], axis=0))
    conformer_operands = (
        zero_bits, row(norm_mix[0]), _resident(conv_w_pw1[0].astype(BF16)), row(conv_b_pw1[0]),
        _resident(conv_taps), row(conv_ln_g[0]), row(conv_ln_b[0]),
        _resident(conv_w_pw2[0].astype(BF16)), row(conv_b_pw2[0]),
        row(norm_ffn[0]), *_ffn_operands(0, w_up_bf16, ffn_w_dw[0], ffn_b_dw[0], w_down_bf16))
    x = _layer_call(
        _conformer_layer_kernel, x, conformer_operands,
        [pltpu.VMEM((SEQ_TILE // 2, D_MODEL), U32),
         pltpu.VMEM((SEQ_TILE // 2, D_MODEL), U32),
         pltpu.VMEM((2, N_LANE_BLOCKS, CONV_HALO + SEQ_TILE, LANES), F32),
         pltpu.VMEM((SEQ_TILE, D_MODEL), F32),
         ] + _ffn_scratch(),
        "conformer_layer", n_inputs=2, skew=2)

    pool_operands = (
        zero_bits, row(norm_mix[1]), _resident(pool_w[0].astype(BF16)), row(pool_b[0]), row(pool_scale[0]),
        row(norm_ffn[1]), *_ffn_operands(1, w_up_bf16, ffn_w_dw[1], ffn_b_dw[1], w_down_bf16),
        row(final_norm))
    return _layer_call(
        _pool_layer_kernel, x, pool_operands,
        [pltpu.VMEM((SEQ_TILE // 2, D_MODEL), U32),
         pltpu.VMEM((N_LANE_BLOCKS, POOL_HALO + SEQ_TILE, LANES), F32),
         pltpu.VMEM((SEQ_TILE, D_MODEL), F32),
         ] + _ffn_scratch(),
        "pool_layer", n_inputs=1, skew=1)
```
